```python
import math
import jax, jax.numpy as jnp
from jax import lax
import numpy as np

D_MODEL = 1024
BATCH = 4
SEQ = 8192
DEPTH = 2

MEM_LEN = 256
RMS_EPS = 1e-6
ROPE_THETA = 500000.0
ROPE_FRACTION = 4
Q_BLOCK = 128
MOBA_HEADS = 6
MOBA_DIM = 64
MOBA_BLOCK = 256
MOBA_TOPK = 3
MOBA_Q_BLOCK = 64
DIFF_HEADS = 4
DIFF_QK_DIM = 32
DIFF_V_DIM = 64
SB_HEADS = 6
SB_DIM = 64
XATTN_HEADS = 4
XATTN_DIM = 128

MOBA_W = MOBA_HEADS * MOBA_DIM
DIFF_QK_W = DIFF_HEADS * 2 * DIFF_QK_DIM
DIFF_W = DIFF_HEADS * DIFF_V_DIM
SB_W = SB_HEADS * SB_DIM
D_MIX = MOBA_W + DIFF_W + SB_W
IN_SPLIT_SIZES = (MOBA_W, MOBA_W, MOBA_W, MOBA_W,
                  DIFF_QK_W, DIFF_QK_W, DIFF_W, DIFF_W,
                  SB_W, SB_W, SB_W, SB_W)
D_IN = 4 * MOBA_W + 2 * DIFF_QK_W + 2 * DIFF_W + 4 * SB_W
XATTN_W = XATTN_HEADS * XATTN_DIM

kernel_name = "hybrid_moba_diff_stickbreak_memxattn"


def rms_norm(x, g):
    xf = x.astype(jnp.float32)
    y = xf * lax.rsqrt(jnp.mean(xf * xf, axis=-1, keepdims=True) + RMS_EPS)
    return (y * g.astype(jnp.float32)).astype(x.dtype)


def to_heads(t, n_heads, d):
    b, s, _ = t.shape
    return t.reshape(b, s, n_heads, d).transpose(0, 2, 1, 3)


def from_heads(t):
    b, h, s, d = t.shape
    return t.transpose(0, 2, 1, 3).reshape(b, s, h * d)


def rope_tables(positions, head_dim):
    rot = head_dim // ROPE_FRACTION
    inv = 1.0 / (ROPE_THETA ** (jnp.arange(0, rot, 2, dtype=jnp.float32) / rot))
    ang = positions.astype(jnp.float32)[..., None] * inv
    return jnp.cos(ang)[:, None], jnp.sin(ang)[:, None]


def apply_partial_rope(x, cos, sin):
    r2 = cos.shape[-1]
    c = cos.astype(x.dtype)
    s = sin.astype(x.dtype)
    x1 = x[..., :r2]
    x2 = x[..., r2:2 * r2]
    return jnp.concatenate([x1 * c - x2 * s, x2 * c + x1 * s, x[..., 2 * r2:]], axis=-1)


def sweep_blocks(fn, seq, qb):
    out = lax.map(fn, jnp.arange(seq // qb))
    n, b, h, _, d = out.shape
    return out.transpose(1, 2, 0, 3, 4).reshape(b, h, n * qb, d)


def moba_attention(q, k, v):
    b, h, s, d = q.shape
    nb = -(-s // MOBA_BLOCK)
    pad = nb * MOBA_BLOCK - s
    kp = jnp.pad(k, ((0, 0), (0, 0), (0, pad), (0, 0)))
    vp = jnp.pad(v, ((0, 0), (0, 0), (0, pad), (0, 0)))
    kb = kp.reshape(b, h, nb, MOBA_BLOCK, d)
    vb = vp.reshape(b, h, nb, MOBA_BLOCK, d)
    k_mean = jnp.mean(kb.astype(jnp.float32), axis=3)
    n_sel = min(MOBA_TOPK, nb)
    scale = d ** -0.5
    bi = jnp.arange(b)[:, None, None, None]
    hi = jnp.arange(h)[None, :, None, None]

    def block(i):
        q0 = i * MOBA_Q_BLOCK
        qi = lax.dynamic_slice_in_dim(q, q0, MOBA_Q_BLOCK, axis=2)
        own = q0 // MOBA_BLOCK
        tq = q0 + jnp.arange(MOBA_Q_BLOCK)
        gate = jnp.einsum('bhqd,bhnd->bhqn', qi.astype(jnp.float32), k_mean)
        gate = jnp.where(jnp.arange(nb) < own, gate, -jnp.inf)
        _, idx = lax.top_k(gate, n_sel)
        valid = idx < own
        ks = kb[bi, hi, idx]
        vs = vb[bi, hi, idx]
        s_sel = jnp.einsum('bhqd,bhqnld->bhqnl', qi, ks).astype(jnp.float32) * scale
        s_sel = jnp.where(valid[..., None], s_sel, -jnp.inf)
        s_sel = s_sel.reshape(b, h, MOBA_Q_BLOCK, n_sel * MOBA_BLOCK)
        ko = lax.dynamic_slice_in_dim(kp, own * MOBA_BLOCK, MOBA_BLOCK, axis=2)
        vo = lax.dynamic_slice_in_dim(vp, own * MOBA_BLOCK, MOBA_BLOCK, axis=2)
        s_own = jnp.einsum('bhqd,bhld->bhql', qi, ko).astype(jnp.float32) * scale
        to = own * MOBA_BLOCK + jnp.arange(MOBA_BLOCK)
        s_own = jnp.where(to[None, :] <= tq[:, None], s_own, -jnp.inf)
        p = jax.nn.softmax(jnp.concatenate([s_sel, s_own], axis=-1), axis=-1)
        p_sel = p[..., :n_sel * MOBA_BLOCK].reshape(b, h, MOBA_Q_BLOCK, n_sel, MOBA_BLOCK).astype(v.dtype)
        p_own = p[..., n_sel * MOBA_BLOCK:].astype(v.dtype)
        return (jnp.einsum('bhqnl,bhqnld->bhqd', p_sel, vs)
                + jnp.einsum('bhql,bhld->bhqd', p_own, vo))

    return sweep_blocks(block, s, MOBA_Q_BLOCK)


def diff_attention(q, k, v, lam, head_norm_g, lam_init):
    b, h2, s, dq = q.shape
    h = h2 // 2
    scale = dq ** -0.5
    kpos = jnp.arange(s)

    def block(i):
        q0 = i * Q_BLOCK
        qi = lax.dynamic_slice_in_dim(q, q0, Q_BLOCK, axis=2)
        sc = jnp.einsum('bhqd,bhkd->bhqk', qi, k).astype(jnp.float32) * scale
        mask = kpos[None, :] <= (q0 + jnp.arange(Q_BLOCK))[:, None]
        p = jax.nn.softmax(jnp.where(mask, sc, -jnp.inf), axis=-1).reshape(b, h, 2, Q_BLOCK, s)
        a = p[:, :, 0] - lam * p[:, :, 1]
        return jnp.einsum('bhqk,bhkd->bhqd', a.astype(v.dtype), v)

    o = sweep_blocks(block, s, Q_BLOCK)
    o = rms_norm(o, head_norm_g)
    return o * (1.0 - lam_init)


def stick_breaking_attention(q, k, v):
    b, h, s, d = q.shape
    scale = d ** -0.5
    kpos = jnp.arange(s)

    def block(i):
        q0 = i * Q_BLOCK
        qi = lax.dynamic_slice_in_dim(q, q0, Q_BLOCK, axis=2)
        z = jnp.einsum('bhqd,bhkd->bhqk', qi, k).astype(jnp.float32) * scale
        mask = kpos[None, :] < (q0 + jnp.arange(Q_BLOCK))[:, None]
        log_beta = jax.nn.log_sigmoid(z)
        log_1m_beta = jnp.where(mask, log_beta - z, 0.0)
        between = lax.cumsum(log_1m_beta, axis=3, reverse=True) - log_1m_beta
        w = jnp.where(mask, jnp.exp(log_beta + between), 0.0)
        return jnp.einsum('bhqk,bhkd->bhqd', w.astype(v.dtype), v)

    return sweep_blocks(block, s, Q_BLOCK)


def hybrid_mixer(hn, rope64, rope32, w_in, w_out, lq1, lk1, lq2, lk2, head_norm_g, lam_init):
    proj = hn @ w_in
    cuts = list(np.cumsum(IN_SPLIT_SIZES)[:-1])
    (mq, mk, mv, mg, dq_, dk_, dv_, dg, sq, sk, sv, sg) = jnp.split(proj, cuts, axis=-1)
    cos64, sin64 = rope64
    a_q = apply_partial_rope(to_heads(mq, MOBA_HEADS, MOBA_DIM), cos64, sin64)
    a_k = apply_partial_rope(to_heads(mk, MOBA_HEADS, MOBA_DIM), cos64, sin64)
    a_o = moba_attention(a_q, a_k, to_heads(mv, MOBA_HEADS, MOBA_DIM))
    cos32, sin32 = rope32
    b_q = apply_partial_rope(to_heads(dq_, 2 * DIFF_HEADS, DIFF_QK_DIM), cos32, sin32)
    b_k = apply_partial_rope(to_heads(dk_, 2 * DIFF_HEADS, DIFF_QK_DIM), cos32, sin32)
    f32 = jnp.float32
    lam = (jnp.exp(jnp.sum(lq1.astype(f32) * lk1.astype(f32)))
           - jnp.exp(jnp.sum(lq2.astype(f32) * lk2.astype(f32))) + lam_init)
    b_o = diff_attention(b_q, b_k, to_heads(dv_, DIFF_HEADS, DIFF_V_DIM), lam, head_norm_g, lam_init)
    c_o = stick_breaking_attention(to_heads(sq, SB_HEADS, SB_DIM), to_heads(sk, SB_HEADS, SB_DIM),
                                   to_heads(sv, SB_HEADS, SB_DIM))
    merged = jnp.concatenate([from_heads(a_o) * jax.nn.silu(mg),
                              from_heads(b_o).astype(hn.dtype) * jax.nn.silu(dg),
                              from_heads(c_o) * jax.nn.silu(sg)], axis=-1)
    return merged @ w_out


def memory_cross_attention(hn, mem_n, w_xq, w_xkv, w_xo):
    q = to_heads(hn @ w_xq, XATTN_HEADS, XATTN_DIM)
    k, v = jnp.split(mem_n @ w_xkv, 2, axis=-1)
    k = to_heads(k, XATTN_HEADS, XATTN_DIM)
    v = to_heads(v, XATTN_HEADS, XATTN_DIM)
    sc = jnp.einsum('bhsd,bhmd->bhsm', q, k).astype(jnp.float32) * (XATTN_DIM ** -0.5)
    p = jax.nn.softmax(sc, axis=-1).astype(v.dtype)
    o = jnp.einsum('bhsm,bhmd->bhsd', p, v)
    return from_heads(o) @ w_xo


def setup_inputs(seed: int = 0) -> dict:
    key = jax.random.key(seed)
    ks = jax.random.split(key, 20)
    f32 = jnp.float32

    def w(k, shape, fan_in):
        return jax.random.normal(k, shape, f32) * (fan_in ** -0.5)

    def gain(k, shape):
        return 1.0 + 0.02 * jax.random.normal(k, shape, f32)

    offsets = jax.random.randint(ks[2], (BATCH, 1), 0, 4096, dtype=jnp.int32)
    positions = offsets + jnp.arange(SEQ, dtype=jnp.int32)[None, :]
    return {
        "x": jax.random.normal(ks[0], (BATCH, SEQ, D_MODEL), f32),
        "mem": jax.random.normal(ks[1], (BATCH, MEM_LEN, D_MODEL), f32),
        "positions": positions,
        "attn_norm_g": gain(ks[3], (DEPTH, D_MODEL)),
        "w_in": w(ks[4], (DEPTH, D_MODEL, D_IN), D_MODEL),
        "w_out": w(ks[5], (DEPTH, D_MIX, D_MODEL), D_MIX),
        "diff_lambda_q1": 0.1 * jax.random.normal(ks[6], (DEPTH, DIFF_QK_DIM), f32),
        "diff_lambda_k1": 0.1 * jax.random.normal(ks[7], (DEPTH, DIFF_QK_DIM), f32),
        "diff_lambda_q2": 0.1 * jax.random.normal(ks[8], (DEPTH, DIFF_QK_DIM), f32),
        "diff_lambda_k2": 0.1 * jax.random.normal(ks[9], (DEPTH, DIFF_QK_DIM), f32),
        "diff_head_norm_g": gain(ks[10], (DEPTH, DIFF_V_DIM)),
        "xattn_norm_g": gain(ks[11], (DEPTH, D_MODEL)),
        "mem_norm_g": gain(ks[12], (DEPTH, D_MODEL)),
        "w_xq": w(ks[13], (DEPTH, D_MODEL, XATTN_W), D_MODEL),
        "w_xkv": w(ks[14], (DEPTH, D_MODEL, 2 * XATTN_W), D_MODEL),
        "w_xo": w(ks[15], (DEPTH, XATTN_W, D_MODEL), XATTN_W),
        "final_norm_g": gain(ks[16], (D_MODEL,)),
    }


def reference(x, mem, positions, attn_norm_g, w_in, w_out, diff_lambda_q1, diff_lambda_k1,
              diff_lambda_q2, diff_lambda_k2, diff_head_norm_g, xattn_norm_g, mem_norm_g,
              w_xq, w_xkv, w_xo, final_norm_g):
    rope64 = rope_tables(positions, MOBA_DIM)
    rope32 = rope_tables(positions, DIFF_QK_DIM)
    h = x
    for l in range(DEPTH):
        lam_init = 0.8 - 0.6 * math.exp(-0.3 * l)
        hn = rms_norm(h, attn_norm_g[l])
        h = h + hybrid_mixer(hn, rope64, rope32, w_in[l], w_out[l],
                             diff_lambda_q1[l], diff_lambda_k1[l], diff_lambda_q2[l], diff_lambda_k2[l],
                             diff_head_norm_g[l], lam_init)
        h = h + memory_cross_attention(rms_norm(h, xattn_norm_g[l]), rms_norm(mem, mem_norm_g[l]),
                                       w_xq[l], w_xkv[l], w_xo[l])
    return rms_norm(h, final_norm_g)
```

```python
import functools
import math

import jax
import jax.numpy as jnp
from jax import lax
from jax.experimental import pallas as pl
from jax.experimental.pallas import tpu as pltpu

F32 = jnp.float32
BF16 = jnp.bfloat16

LANES = 128
RMS_EPS = 1e-6
ROPE_THETA = 500000.0
ROPE_FRACTION = 4

MOBA_HEADS, MOBA_DIM, MOBA_BLOCK, MOBA_TOPK = 6, 64, 256, 3
DIFF_HEADS, DIFF_QK_DIM, DIFF_V_DIM = 4, 32, 64
SB_HEADS, SB_DIM = 6, 64
XATTN_HEADS, XATTN_DIM = 4, 128

MOBA_W = MOBA_HEADS * MOBA_DIM
DIFF_QK_W = DIFF_HEADS * 2 * DIFF_QK_DIM
DIFF_W = DIFF_HEADS * DIFF_V_DIM
SB_W = SB_HEADS * SB_DIM
XATTN_W = XATTN_HEADS * XATTN_DIM

MOBA_G = MOBA_W // LANES
DIFF_G = DIFF_QK_W // LANES
SB_G = SB_W // LANES

LOG2E = math.log2(math.e)
NEG_BIAS = -32768.0
SB_CUTOFF = -100.0

_QK, _V, _GATE = 0, 1, 2
_GROUPS = (
    [(_QK, i, 64) for i in range(0, 3)] + [(_QK, i, 64) for i in range(3, 6)]
    + [(_V, i, 0) for i in range(0, 3)] + [(_GATE, i, 0) for i in range(0, 3)]
    + [(_QK, i, 32) for i in range(6, 8)] + [(_QK, i, 32) for i in range(8, 10)]
    + [(_V, i, 0) for i in range(3, 5)] + [(_GATE, i, 0) for i in range(3, 5)]
    + [(_QK, i, 0) for i in range(10, 13)] + [(_QK, i, 0) for i in range(13, 16)]
    + [(_V, i, 0) for i in range(5, 8)] + [(_GATE, i, 0) for i in range(5, 8)]
)
N_QK, N_V, N_GATE = 16, 8, 8
QK_MOBA_Q, QK_MOBA_K, QK_DIFF_Q, QK_DIFF_K, QK_SB_Q, QK_SB_K = 0, 3, 6, 8, 10, 13
V_MOBA, V_DIFF, V_SB = 0, 3, 5
KMEAN_ROWS = 8

PROJ_TS = 512
ATT_TQ = 256
ATT_TK = 256
SB_TK = 128
OUT_TS = 512


def _dot(a, b):
    return jnp.dot(a, b, preferred_element_type=F32)


def _dot_nt(a, b):
    return lax.dot_general(a, b, (((1,), (1,)), ((), ())), preferred_element_type=F32)


def _rms(x, g):
    return x * lax.rsqrt(jnp.mean(x * x, axis=-1, keepdims=True) + RMS_EPS) * g


def _silu(x):
    return x / (1.0 + jnp.exp(-x))


def _lane_mask(lo, hi, dtype):
    lane = lax.broadcasted_iota(jnp.int32, (1, LANES), 1)
    return ((lane >= lo) & (lane < hi)).astype(F32).astype(dtype)


def _proj_kernel(x_ref, g_ref, w_ref, c64_ref, s64_ref, c32_ref, s32_ref,
                 qk_ref, v_ref, gate_ref, kmean_ref, *, ts):
    hn = _rms(x_ref[...], g_ref[...]).astype(BF16)
    lane = lax.broadcasted_iota(jnp.int32, (ts, LANES), 1)
    first = {64: (lane & 63) < 8, 32: (lane & 31) < 4}
    half_rot = {64: 8, 32: 4}
    tabs = {64: (c64_ref, s64_ref), 32: (c32_ref, s32_ref)}
    kmean_ref[...] = jnp.zeros(kmean_ref.shape, F32)
    n_chunks = w_ref.shape[1] // (2 * LANES)
    for j in range(n_chunks):
        r2 = _dot(hn, w_ref[:, j * 2 * LANES:(j + 1) * 2 * LANES])
        for half in range(2):
            dst, idx, rope = _GROUPS[2 * j + half]
            r = r2[:, half * LANES:(half + 1) * LANES]
            if rope:
                sh = half_rot[rope]
                c_ref, s_ref = tabs[rope]
                rolled = jnp.where(first[rope], pltpu.roll(r, LANES - sh, 1), pltpu.roll(r, sh, 1))
                r = r * c_ref[...] + rolled * s_ref[...]
            if dst == _QK:
                qk_ref[idx] = r.astype(BF16)
                if QK_MOBA_K <= idx < QK_MOBA_K + MOBA_G:
                    col = (idx - QK_MOBA_K) * LANES
                    for blk in range(ts // MOBA_BLOCK):
                        kmean_ref[blk:blk + 1, col:col + LANES] = jnp.mean(
                            r[blk * MOBA_BLOCK:(blk + 1) * MOBA_BLOCK], axis=0, keepdims=True)
            elif dst == _V:
                v_ref[idx] = r.astype(BF16)
            else:
                gate_ref[idx] = r


def _project(h, g, w_bf16, tabs, ts=PROJ_TS):
    B, S, D = h.shape
    d_in = w_bf16.shape[1]
    nT = S // ts
    tab_spec = pl.BlockSpec((None, ts, LANES), lambda b, t: (b, t, 0))
    return pl.pallas_call(
        functools.partial(_proj_kernel, ts=ts),
        grid=(B, nT),
        in_specs=[
            pl.BlockSpec((None, ts, D), lambda b, t: (b, t, 0)),
            pl.BlockSpec((1, D), lambda b, t: (0, 0)),
            pl.BlockSpec((D, d_in), lambda b, t: (0, 0)),
            tab_spec, tab_spec, tab_spec, tab_spec,
        ],
        out_specs=[
            pl.BlockSpec((N_QK, None, ts, LANES), lambda b, t: (0, b, t, 0)),
            pl.BlockSpec((N_V, None, ts, LANES), lambda b, t: (0, b, t, 0)),
            pl.BlockSpec((N_GATE, None, ts, LANES), lambda b, t: (0, b, t, 0)),
            pl.BlockSpec((None, None, KMEAN_ROWS, MOBA_W), lambda b, t: (b, t, 0, 0)),
        ],
        out_shape=[
            jax.ShapeDtypeStruct((N_QK, B, S, LANES), BF16),
            jax.ShapeDtypeStruct((N_V, B, S, LANES), BF16),
            jax.ShapeDtypeStruct((N_GATE, B, S, LANES), F32),
            jax.ShapeDtypeStruct((B, nT, KMEAN_ROWS, MOBA_W), F32),
        ],
        compiler_params=pltpu.CompilerParams(
            dimension_semantics=("arbitrary", "arbitrary"), vmem_limit_bytes=48 * 1024 * 1024),
        name="proj",
    )(h, g.reshape(1, D), w_bf16, *tabs)


def _softmax_update(s, vt, m, l, acc_ref, slot, c):
    m_new = jnp.maximum(m, jnp.max(s, axis=0, keepdims=True))
    alpha = jnp.exp2((m - m_new) * c)
    p = jnp.exp2((s - m_new) * c)
    l_new = alpha * l + jnp.sum(p, axis=0, keepdims=True)
    acc_ref[slot] = alpha * acc_ref[slot] + _dot(vt, p.astype(BF16))
    return m_new, l_new


def _causal_mask(tk, tq):
    key = lax.broadcasted_iota(jnp.int32, (tk, tq), 0)
    qry = lax.broadcasted_iota(jnp.int32, (tk, tq), 1)
    return key, qry


def _moba_kernel(q_ref, k_ref, vt_ref, km_ref, g_ref, o_ref, acc_ref):
    i = pl.program_id(2)
    tq = q_ref.shape[0]
    c = MOBA_DIM ** -0.5 * LOG2E
    q = q_ref[...]
    lane = lax.broadcasted_iota(jnp.int32, (tq, LANES), 1)
    lane_f = lane.astype(F32)
    lane_row = lax.broadcasted_iota(jnp.int32, (1, LANES), 1)
    key, qry = _causal_mask(ATT_TK, tq)
    causal = key <= qry

    q_aug, head_m, bias_base = [], [], []
    for h in range(2):
        hm = _lane_mask(64 * h, 64 * (h + 1), BF16)
        base = 64 * (1 - h)
        qh = q * hm
        km = km_ref[h]
        km_hi = km.astype(BF16)
        km_lo = (km - km_hi.astype(F32)).astype(BF16)
        gate = _dot_nt(qh, km_hi) + _dot_nt(qh, km_lo)
        cand = (lane >= base) & (lane < base + i)
        g = jnp.where(cand, gate, -jnp.inf)
        unsel = jnp.ones((tq, LANES), F32)
        for _ in range(MOBA_TOPK):
            mx = jnp.max(g, axis=1, keepdims=True)
            first = jnp.min(jnp.where(g == mx, lane_f, 4.0 * LANES), axis=1, keepdims=True)
            pick = (lane_f == first) & (mx > -jnp.inf)
            unsel = jnp.where(pick, 0.0, unsel)
            g = jnp.where(pick, -jnp.inf, g)
        in_head = (lane >= 64 * h) & (lane < 64 * (h + 1))
        bias = jnp.where(in_head, 0.0, unsel * NEG_BIAS)
        q_aug.append(qh + bias.astype(BF16))
        head_m.append(hm)
        bias_base.append(base)

    acc_ref[...] = jnp.zeros(acc_ref.shape, F32)
    k_own = k_ref[i]
    vt_own = vt_ref[i]
    stats = []
    for h in range(2):
        s = _dot_nt(k_own * head_m[h], q_aug[h])
        s = jnp.where(causal, s, -jnp.inf)
        m0 = jnp.full((1, tq), -jnp.inf, F32)
        l0 = jnp.zeros((1, tq), F32)
        stats.extend(_softmax_update(s, vt_own[64 * h:64 * (h + 1), :], m0, l0, acc_ref, h, c))

    def body(j, st):
        k_blk = k_ref[j]
        vt_blk = vt_ref[j]
        out = []
        for h in range(2):
            onehot = (lane_row == bias_base[h] + j).astype(F32).astype(BF16)
            k_aug = k_blk * head_m[h] + onehot
            s = _dot_nt(k_aug, q_aug[h])
            out.extend(_softmax_update(s, vt_blk[64 * h:64 * (h + 1), :],
                                       st[2 * h], st[2 * h + 1], acc_ref, h, c))
        return tuple(out)

    m0, l0, m1, l1 = lax.fori_loop(0, i, body, tuple(stats))
    ot = jnp.concatenate([acc_ref[0] / l0, acc_ref[1] / l1], axis=0)
    o_ref[...] = (ot.T * _silu(g_ref[...])).astype(o_ref.dtype)


def _moba(qk, vt, km_pad, gates, B, S):
    tq, tk = ATT_TQ, ATT_TK
    nb = S // tk
    k5 = qk.reshape(N_QK, B, nb, tk, LANES)
    return pl.pallas_call(
        _moba_kernel,
        grid=(B, MOBA_G, S // tq),
        in_specs=[
            pl.BlockSpec((None, None, tq, LANES), lambda b, p, i: (QK_MOBA_Q + p, b, i, 0)),
            pl.BlockSpec((None, None, nb, tk, LANES), lambda b, p, i: (QK_MOBA_K + p, b, 0, 0, 0)),
            pl.BlockSpec((None, None, nb, LANES, tk), lambda b, p, i: (p, b, 0, 0, 0)),
            pl.BlockSpec((None, None, 2, LANES, LANES), lambda b, p, i: (b, p, 0, 0, 0)),
            pl.BlockSpec((None, None, tq, LANES), lambda b, p, i: (p, b, i, 0)),
        ],
        out_specs=pl.BlockSpec((None, tq, LANES), lambda b, p, i: (b, i, p)),
        out_shape=jax.ShapeDtypeStruct((B, S, MOBA_W), BF16),
        scratch_shapes=[pltpu.VMEM((2, MOBA_DIM, tq), F32)],
        compiler_params=pltpu.CompilerParams(
            dimension_semantics=("arbitrary", "arbitrary", "arbitrary"),
            vmem_limit_bytes=40 * 1024 * 1024),
        name="moba",
    )(qk, k5, vt, km_pad, gates)


def _diff_kernel(q_ref, k_ref, vt_ref, lam_ref, hg_ref, g_ref, o_ref, acc_ref, *, lam_init):
    i = pl.program_id(2)
    tq = q_ref.shape[0]
    c = DIFF_QK_DIM ** -0.5 * LOG2E
    q = q_ref[...]
    key, qry = _causal_mask(ATT_TK, tq)
    causal = key <= qry
    qs = [q * _lane_mask(32 * sub, 32 * (sub + 1), BF16) for sub in range(4)]

    acc_ref[...] = jnp.zeros(acc_ref.shape, F32)
    k_own = k_ref[i]
    vt_own = vt_ref[i]
    stats = []
    for sub in range(4):
        hd = sub // 2
        s = jnp.where(causal, _dot_nt(k_own, qs[sub]), -jnp.inf)
        m0 = jnp.full((1, tq), -jnp.inf, F32)
        l0 = jnp.zeros((1, tq), F32)
        stats.extend(_softmax_update(s, vt_own[64 * hd:64 * (hd + 1), :], m0, l0, acc_ref, sub, c))

    def body(j, st):
        k_blk = k_ref[j]
        vt_blk = vt_ref[j]
        out = []
        for sub in range(4):
            hd = sub // 2
            s = _dot_nt(k_blk, qs[sub])
            out.extend(_softmax_update(s, vt_blk[64 * hd:64 * (hd + 1), :],
                                       st[2 * sub], st[2 * sub + 1], acc_ref, sub, c))
        return tuple(out)

    st = lax.fori_loop(0, i, body, tuple(stats))
    lam_rows = lam_ref[...]
    lam = (jnp.exp(jnp.sum(lam_rows[0:1] * lam_rows[1:2], axis=1, keepdims=True))
           - jnp.exp(jnp.sum(lam_rows[2:3] * lam_rows[3:4], axis=1, keepdims=True)) + lam_init)
    heads = []
    for hd in range(2):
        o = acc_ref[2 * hd] / st[4 * hd + 1] - lam * (acc_ref[2 * hd + 1] / st[4 * hd + 3])
        heads.append(o * lax.rsqrt(jnp.mean(o * o, axis=0, keepdims=True) + RMS_EPS))
    o = jnp.concatenate(heads, axis=0).T * hg_ref[...] * (1.0 - lam_init)
    o_ref[...] = (o * _silu(g_ref[...])).astype(o_ref.dtype)


def _diff(qk, vt, lam_rows, head_g, gates, lam_init, B, S):
    tq, tk = ATT_TQ, ATT_TK
    nb = S // tk
    k5 = qk.reshape(N_QK, B, nb, tk, LANES)
    return pl.pallas_call(
        functools.partial(_diff_kernel, lam_init=lam_init),
        grid=(B, DIFF_G, S // tq),
        in_specs=[
            pl.BlockSpec((None, None, tq, LANES), lambda b, p, i: (QK_DIFF_Q + p, b, i, 0)),
            pl.BlockSpec((None, None, nb, tk, LANES), lambda b, p, i: (QK_DIFF_K + p, b, 0, 0, 0)),
            pl.BlockSpec((None, None, nb, LANES, tk), lambda b, p, i: (p, b, 0, 0, 0)),
            pl.BlockSpec((4, LANES), lambda b, p, i: (0, 0)),
            pl.BlockSpec((1, LANES), lambda b, p, i: (0, 0)),
            pl.BlockSpec((None, None, tq, LANES), lambda b, p, i: (MOBA_G + p, b, i, 0)),
        ],
        out_specs=pl.BlockSpec((None, tq, LANES), lambda b, p, i: (b, i, p)),
        out_shape=jax.ShapeDtypeStruct((B, S, DIFF_W), BF16),
        scratch_shapes=[pltpu.VMEM((4, DIFF_V_DIM, tq), F32)],
        compiler_params=pltpu.CompilerParams(
            dimension_semantics=("arbitrary", "arbitrary", "arbitrary"),
            vmem_limit_bytes=40 * 1024 * 1024),
        name="diff",
    )(qk, k5, vt, lam_rows, head_g, gates)


def _sb_kernel(q_ref, k_ref, vt_ref, g_ref, o_ref, acc_ref):
    i = pl.program_id(2)
    tq = q_ref.shape[0]
    tk = SB_TK
    per_tile = tq // tk
    q = q_ref[...] * (SB_DIM ** -0.5)
    row = lax.broadcasted_iota(jnp.int32, (tk, tk), 0)
    col = lax.broadcasted_iota(jnp.int32, (tk, tk), 1)
    upper = (col > row).astype(F32).astype(BF16)
    key, qry = _causal_mask(tk, tq)

    acc_ref[...] = jnp.zeros(acc_ref.shape, F32)
    for h in range(2):
        qh = q * _lane_mask(64 * h, 64 * (h + 1), BF16)

        def step(kb, carry, valid, h=h, qh=qh):
            z = _dot_nt(k_ref[kb], qh)
            sp = jnp.maximum(z, 0.0) + jnp.log1p(jnp.exp(-jnp.abs(z)))
            lg = -sp
            if valid is not None:
                lg = jnp.where(valid, lg, 0.0)
            lg_hi = lg.astype(BF16)
            lg_lo = (lg - lg_hi.astype(F32)).astype(BF16)
            between = carry + _dot(upper, lg_hi) + _dot(upper, lg_lo)
            w = jnp.exp((z - sp) + between)
            if valid is not None:
                w = jnp.where(valid, w, 0.0)
            acc_ref[h] += _dot(vt_ref[kb][64 * h:64 * (h + 1), :], w.astype(BF16))
            return carry + jnp.sum(lg, axis=0, keepdims=True)

        carry = jnp.zeros((1, tq), F32)
        for d in reversed(range(per_tile)):
            valid = (key + d * tk) < qry
            carry = step(i * per_tile + d, carry, valid)

        def cond(st):
            return (st[0] >= 0) & (st[1] > SB_CUTOFF)

        def body(st):
            nc = step(st[0], st[2], None)
            return st[0] - 1, jnp.max(nc), nc

        lax.while_loop(cond, body, (i * per_tile - 1, jnp.max(carry), carry))

    ot = jnp.concatenate([acc_ref[0], acc_ref[1]], axis=0)
    o_ref[...] = (ot.T * _silu(g_ref[...])).astype(o_ref.dtype)


def _sb(qk, vt, gates, B, S):
    tq, tk = ATT_TQ, SB_TK
    nb = S // tk
    k5 = qk.reshape(N_QK, B, nb, tk, LANES)
    return pl.pallas_call(
        _sb_kernel,
        grid=(B, SB_G, S // tq),
        in_specs=[
            pl.BlockSpec((None, None, tq, LANES), lambda b, p, i: (QK_SB_Q + p, b, i, 0)),
            pl.BlockSpec((None, None, nb, tk, LANES), lambda b, p, i: (QK_SB_K + p, b, 0, 0, 0)),
            pl.BlockSpec((None, None, nb, LANES, tk), lambda b, p, i: (p, b, 0, 0, 0)),
            pl.BlockSpec((None, None, tq, LANES), lambda b, p, i: (MOBA_G + DIFF_G + p, b, i, 0)),
        ],
        out_specs=pl.BlockSpec((None, tq, LANES), lambda b, p, i: (b, i, p)),
        out_shape=jax.ShapeDtypeStruct((B, S, SB_W), BF16),
        scratch_shapes=[pltpu.VMEM((2, SB_DIM, tq), F32)],
        compiler_params=pltpu.CompilerParams(
            dimension_semantics=("arbitrary", "arbitrary", "arbitrary"),
            vmem_limit_bytes=40 * 1024 * 1024),
        name="stickbreak",
    )(qk, k5, vt, gates)


def _memkv_kernel(mem_ref, g_ref, w_ref, k_ref, v_ref):
    mn = _rms(mem_ref[...], g_ref[...]).astype(BF16)
    kv = _dot(mn, w_ref[...])
    k_ref[...] = kv[:, :XATTN_W].astype(BF16)
    v_ref[...] = kv[:, XATTN_W:].astype(BF16)


def _memkv(mem, g, w_bf16):
    B, M, D = mem.shape
    return pl.pallas_call(
        _memkv_kernel,
        grid=(B,),
        in_specs=[
            pl.BlockSpec((None, M, D), lambda b: (b, 0, 0)),
            pl.BlockSpec((1, D), lambda b: (0, 0)),
            pl.BlockSpec((D, 2 * XATTN_W), lambda b: (0, 0)),
        ],
        out_specs=[pl.BlockSpec((None, M, XATTN_W), lambda b: (b, 0, 0))] * 2,
        out_shape=[jax.ShapeDtypeStruct((B, M, XATTN_W), BF16)] * 2,
        compiler_params=pltpu.CompilerParams(dimension_semantics=("arbitrary",)),
        name="memkv",
    )(mem, g.reshape(1, D), w_bf16)


def _out_kernel(h_ref, a_ref, b_ref, c_ref, wo_ref, xg_ref, wq_ref, mk_ref, mv_ref, wxo_ref,
                fg_ref, o_ref, *, final):
    wo = wo_ref
    mix = (_dot(a_ref[...], wo[0:MOBA_W, :])
           + _dot(b_ref[...], wo[MOBA_W:MOBA_W + DIFF_W, :])
           + _dot(c_ref[...], wo[MOBA_W + DIFF_W:, :]))
    h1 = h_ref[...] + mix
    hn = _rms(h1, xg_ref[...]).astype(BF16)
    q = _dot(hn, wq_ref[...]).astype(BF16)
    c = XATTN_DIM ** -0.5 * LOG2E
    outs = []
    for hd in range(XATTN_HEADS):
        sl = slice(hd * XATTN_DIM, (hd + 1) * XATTN_DIM)
        sc = _dot_nt(q[:, sl], mk_ref[:, sl])
        m = jnp.max(sc, axis=-1, keepdims=True)
        p = jnp.exp2((sc - m) * c)
        l = jnp.sum(p, axis=-1, keepdims=True)
        outs.append((_dot(p.astype(BF16), mv_ref[:, sl]) / l).astype(BF16))
    o = jnp.concatenate(outs, axis=-1)
    h2 = h1 + _dot(o, wxo_ref[...])
    if final:
        h2 = _rms(h2, fg_ref[...])
    o_ref[...] = h2


def _mix_out(h, a, b, c, wo, xg, wq, mk, mv, wxo, fg, final, ts=OUT_TS):
    B, S, D = h.shape
    M = mk.shape[1]
    row = lambda w: pl.BlockSpec((None, ts, w), lambda bb, t: (bb, t, 0))
    full = lambda r, w: pl.BlockSpec((r, w), lambda bb, t: (0, 0))
    return pl.pallas_call(
        functools.partial(_out_kernel, final=final),
        grid=(B, S // ts),
        in_specs=[
            row(D), row(MOBA_W), row(DIFF_W), row(SB_W),
            full(D, D), full(1, D), full(D, XATTN_W),
            pl.BlockSpec((None, M, XATTN_W), lambda bb, t: (bb, 0, 0)),
            pl.BlockSpec((None, M, XATTN_W), lambda bb, t: (bb, 0, 0)),
            full(XATTN_W, D), full(1, D),
        ],
        out_specs=row(D),
        out_shape=jax.ShapeDtypeStruct((B, S, D), F32),
        compiler_params=pltpu.CompilerParams(
            dimension_semantics=("arbitrary", "arbitrary"), vmem_limit_bytes=40 * 1024 * 1024),
        name="mix_out",
    )(h, a, b, c, wo, xg.reshape(1, D), wq, mk, mv, wxo, fg.reshape(1, D))


def _rope_tables(positions, head_dim):
    rot = head_dim // ROPE_FRACTION
    r2 = rot // 2
    inv = 1.0 / (ROPE_THETA ** (jnp.arange(0, rot, 2, dtype=F32) / rot))
    ang = positions.astype(F32)[..., None] * inv
    cos, sin = jnp.cos(ang), jnp.sin(ang)
    B, S = positions.shape
    pad = head_dim - 2 * r2
    c_head = jnp.concatenate([cos, cos, jnp.ones((B, S, pad), F32)], axis=-1)
    s_head = jnp.concatenate([-sin, sin, jnp.zeros((B, S, pad), F32)], axis=-1)
    reps = LANES // head_dim
    return jnp.tile(c_head, (1, 1, reps)), jnp.tile(s_head, (1, 1, reps))


def _blocked_t(v, tk):
    g, B, S, w = v.shape
    return v.reshape(g, B, S // tk, tk, w).swapaxes(-1, -2)


def _kmean_pad(kmean_tiles, B, S):
    nb = S // MOBA_BLOCK
    per = PROJ_TS // MOBA_BLOCK
    km = kmean_tiles[:, :, :per, :].reshape(B, nb, MOBA_G, LANES).transpose(0, 2, 1, 3)
    lane = jnp.arange(LANES)
    out = []
    for h in range(2):
        hm = ((lane >= 64 * h) & (lane < 64 * (h + 1))).astype(F32)
        base = 64 * (1 - h)
        out.append(jnp.pad(km * hm, ((0, 0), (0, 0), (base, LANES - base - nb), (0, 0))))
    return jnp.stack(out, axis=2)


def kernel(x, mem, positions, attn_norm_g, w_in, w_out, diff_lambda_q1, diff_lambda_k1,
           diff_lambda_q2, diff_lambda_k2, diff_head_norm_g, xattn_norm_g, mem_norm_g,
           w_xq, w_xkv, w_xo, final_norm_g):
    B, S, D = x.shape
    depth = w_in.shape[0]
    tabs = _rope_tables(positions, MOBA_DIM) + _rope_tables(positions, DIFF_QK_DIM)
    h = x
    for l in range(depth):
        lam_init = 0.8 - 0.6 * math.exp(-0.3 * l)
        qk, v, gates, kmean_tiles = _project(h, attn_norm_g[l], w_in[l].astype(BF16), tabs)
        km_pad = _kmean_pad(kmean_tiles, B, S)
        a = _moba(qk, _blocked_t(v[V_MOBA:V_MOBA + MOBA_G], ATT_TK), km_pad, gates, B, S)
        lam_rows = jnp.pad(jnp.stack([diff_lambda_q1[l], diff_lambda_k1[l],
                                      diff_lambda_q2[l], diff_lambda_k2[l]]),
                           ((0, 0), (0, LANES - DIFF_QK_DIM)))
        head_g = jnp.tile(diff_head_norm_g[l], LANES // DIFF_V_DIM).reshape(1, LANES)
        bo = _diff(qk, _blocked_t(v[V_DIFF:V_DIFF + DIFF_G], ATT_TK), lam_rows, head_g, gates,
                   lam_init, B, S)
        co = _sb(qk, _blocked_t(v[V_SB:V_SB + SB_G], SB_TK), gates, B, S)
        mk, mv = _memkv(mem, mem_norm_g[l], w_xkv[l].astype(BF16))
        h = _mix_out(h, a, bo, co, w_out[l].astype(BF16), xattn_norm_g[l], w_xq[l].astype(BF16),
                     mk, mv, w_xo[l].astype(BF16), final_norm_g, final=(l == depth - 1))
    return h
```
